```python
import jax, jax.numpy as jnp
from jax import lax
import numpy as np

D_MODEL = 1024
BATCH = 16
SEQ = 256
DEPTH = 1
DEC_BATCH = 8
DEC_SEQ = 1024
PAST_LEN = 256

GRID_W = 64
N_HEADS = 8
KV_HEADS = 2
HEAD_DIM = 128
Q_GROUP = N_HEADS // KV_HEADS
ATTN_WIDTH = N_HEADS * HEAD_DIM
KV_WIDTH = KV_HEADS * HEAD_DIM
WINDOW = 128
BLOCK = 128
ROPE_BASE = 10000.0
ROPE_AXIS_FREQS = HEAD_DIM // 4
LRU_WIDTH = 1024
LRU_BLOCKS = 8
LRU_BLOCK_W = LRU_WIDTH // LRU_BLOCKS
LRU_C = 8.0
CONV_W = 4
CONV_PAD_LEFT = 2
PEER_HEADS = 8
N_KEYS = 128
N_EXPERTS = N_KEYS * N_KEYS
PK_DIM = 256
PK_HALF = PK_DIM // 2
PEER_TOPK = 16
PEER_CHUNK = 128
NORM_EPS = 1e-6
NEG_INF = -1e30
IN_COLS = ATTN_WIDTH + 2 * KV_WIDTH + 2 * LRU_WIDTH + 2 * D_MODEL
IN_SPLITS = (ATTN_WIDTH,
             ATTN_WIDTH + KV_WIDTH,
             ATTN_WIDTH + 2 * KV_WIDTH,
             ATTN_WIDTH + 2 * KV_WIDTH + LRU_WIDTH,
             ATTN_WIDTH + 2 * KV_WIDTH + 2 * LRU_WIDTH,
             ATTN_WIDTH + 2 * KV_WIDTH + 2 * LRU_WIDTH + D_MODEL)

kernel_name = "hybrid_diffusion_prefix_step"


def rmsnorm(x, g):
    xf = x.astype(jnp.float32)
    y = xf * lax.rsqrt(jnp.mean(xf * xf, axis=-1, keepdims=True) + NORM_EPS)
    return (y * g.astype(jnp.float32)).astype(x.dtype)


def adaln(cond, w_ada, b_ada):
    m = jax.nn.silu(cond) @ w_ada + b_ada
    return [t[:, None, :] for t in jnp.split(m, 6, axis=-1)]


def rope_2d(x):
    S = x.shape[1]
    rows = S // GRID_W
    row = jnp.repeat(jnp.arange(rows, dtype=jnp.float32), GRID_W)
    col = jnp.tile(jnp.arange(GRID_W, dtype=jnp.float32), rows)
    inv = ROPE_BASE ** (-jnp.arange(ROPE_AXIS_FREQS, dtype=jnp.float32) / ROPE_AXIS_FREQS)
    ang = jnp.concatenate([row[:, None] * inv, col[:, None] * inv], axis=-1)
    cos = jnp.cos(ang)[None, :, None, :].astype(x.dtype)
    sin = jnp.sin(ang)[None, :, None, :].astype(x.dtype)
    x1, x2 = jnp.split(x, 2, axis=-1)
    return jnp.concatenate([x1 * cos - x2 * sin, x1 * sin + x2 * cos], axis=-1)


def attend_block(q, k, v, mask, sink):
    s = jnp.einsum('bqkgd,bskd->bkgqs', q, k).astype(jnp.float32) * (HEAD_DIM ** -0.5)
    if mask is not None:
        s = jnp.where(mask, s, NEG_INF)
    sink_col = jnp.broadcast_to(sink.astype(jnp.float32).reshape(1, KV_HEADS, Q_GROUP, 1, 1),
                                s.shape[:-1] + (1,))
    p = jax.nn.softmax(jnp.concatenate([s, sink_col], axis=-1), axis=-1)[..., :-1]
    o = jnp.einsum('bkgqs,bskd->bqkgd', p.astype(v.dtype), v)
    return o.reshape(o.shape[0], o.shape[1], ATTN_WIDTH)


def context_attention(q, k, v, sink):
    B, S = q.shape[:2]
    nb = S // BLOCK
    qb = jnp.moveaxis(q.reshape(B, nb, BLOCK, KV_HEADS, Q_GROUP, HEAD_DIM), 1, 0)
    o = lax.map(lambda qi: attend_block(qi, k, v, None, sink), qb)
    return jnp.moveaxis(o, 0, 1).reshape(B, S, ATTN_WIDTH)


def latent_attention(q, k, v, k_ctx, v_ctx, sink):
    B, S = q.shape[:2]
    nb = S // BLOCK
    Sc = k_ctx.shape[1]

    def windows(t):
        tp = jnp.pad(t, ((0, 0), (BLOCK, BLOCK), (0, 0), (0, 0)))
        tp = tp.reshape(B, nb + 2, BLOCK, KV_HEADS, HEAD_DIM)
        w = jnp.concatenate([tp[:, :-2], tp[:, 1:-1], tp[:, 2:]], axis=2)
        return jnp.moveaxis(w, 1, 0)

    kw, vw = windows(k), windows(v)
    qb = jnp.moveaxis(q.reshape(B, nb, BLOCK, KV_HEADS, Q_GROUP, HEAD_DIM), 1, 0)
    qi_idx = jnp.arange(BLOCK)[:, None]
    kj_idx = jnp.arange(3 * BLOCK)[None, :]
    band = jnp.abs(qi_idx + BLOCK - kj_idx) <= WINDOW
    ctx_ok = jnp.ones((BLOCK, Sc), dtype=bool)

    def one(args):
        qi, ki, vi, n = args
        kpos = (n - 1) * BLOCK + kj_idx
        valid = band & (kpos >= 0) & (kpos < S)
        mask = jnp.concatenate([valid, ctx_ok], axis=-1)
        keys = jnp.concatenate([ki, k_ctx], axis=1)
        vals = jnp.concatenate([vi, v_ctx], axis=1)
        return attend_block(qi, keys, vals, mask, sink)

    o = lax.map(one, (qb, kw, vw, jnp.arange(nb)))
    return jnp.moveaxis(o, 0, 1).reshape(B, S, ATTN_WIDTH)


def dwconv(x, w, b):
    S = x.shape[1]
    xp = jnp.pad(x, ((0, 0), (CONV_PAD_LEFT, CONV_W - 1 - CONV_PAD_LEFT), (0, 0)))
    y = b
    for t in range(CONV_W):
        y = y + xp[:, t:t + S] * w[t]
    return y


def blockdiag(x, w):
    xs = x.reshape(x.shape[:-1] + (LRU_BLOCKS, LRU_BLOCK_W))
    return jnp.einsum('bsnd,nde->bsne', xs, w).reshape(x.shape)


def _lin_combine(left, right):
    a1, b1 = left
    a2, b2 = right
    return a1 * a2, a2 * b1 + b2


def rglru_dir(xc, w_a, b_a, w_i, b_i, lam, h0, reverse):
    xf = xc.astype(jnp.float32)
    if reverse:
        xf = jnp.flip(xf, axis=1)
    r = jax.nn.sigmoid(blockdiag(xf, w_a.astype(jnp.float32)) + b_a.astype(jnp.float32))
    ig = jax.nn.sigmoid(blockdiag(xf, w_i.astype(jnp.float32)) + b_i.astype(jnp.float32))
    log_a = -LRU_C * r * jax.nn.softplus(-lam.astype(jnp.float32))
    a = jnp.exp(log_a)
    bterm = jnp.sqrt(-jnp.expm1(2.0 * log_a)) * ig * xf
    bterm = bterm.at[:, 0].add(a[:, 0] * h0.astype(jnp.float32))
    _, h = lax.associative_scan(_lin_combine, (a, bterm), axis=1)
    final = h[:, -1]
    if reverse:
        h = jnp.flip(h, axis=1)
    return h.astype(xc.dtype), final.astype(xc.dtype)


def rglru_branch(xr, conv_w, conv_b, w_a, b_a, w_i, b_i, lam, h0_f, h0_b):
    xc = dwconv(xr, conv_w, conv_b)
    hf, hf_last = rglru_dir(xc, w_a[0], b_a[0], w_i[0], b_i[0], lam[0], h0_f, False)
    hb, hb_last = rglru_dir(xc, w_a[1], b_a[1], w_i[1], b_i[1], lam[1], h0_b, True)
    return hf + hb, hf_last, hb_last


def peer(h, w_q, sub_keys, u_tab, v_tab):
    shape = h.shape
    xt = h.reshape(-1, PEER_CHUNK, D_MODEL)

    def chunk(x):
        q = (x @ w_q).reshape(PEER_CHUNK, PEER_HEADS, 2, PK_HALF)
        s = jnp.einsum('chpd,hpnd->chpn', q, sub_keys).astype(jnp.float32)
        st, it = lax.top_k(s, PEER_TOPK)
        cand = (st[:, :, 0, :, None] + st[:, :, 1, None, :]).reshape(PEER_CHUNK, PEER_HEADS, -1)
        cidx = (it[:, :, 0, :, None] * N_KEYS + it[:, :, 1, None, :]).reshape(PEER_CHUNK, PEER_HEADS, -1)
        best, pos = lax.top_k(cand, PEER_TOPK)
        eidx = jnp.take_along_axis(cidx, pos, axis=-1)
        g = jax.nn.softmax(best, axis=-1)
        act = jax.nn.gelu(jnp.einsum('chkd,cd->chk', u_tab[eidx], x).astype(jnp.float32))
        return jnp.einsum('chk,chkd->cd', (g * act).astype(x.dtype), v_tab[eidx])

    return lax.map(chunk, xt).reshape(shape)


def trunk_layer(x, cond, p, cache):
    sh_a, sc_a, g_a, sh_f, sc_f, g_f = adaln(cond, p['w_ada'], p['b_ada'])
    B, S = x.shape[:2]
    h = rmsnorm(x, p['norm_attn_g']) * (1.0 + sc_a) + sh_a
    z = h @ p['w_in']
    q, k, v, xr, xg, ga, gr = jnp.split(z, IN_SPLITS, axis=-1)
    q = q.reshape(B, S, N_HEADS, HEAD_DIM)
    k = k.reshape(B, S, KV_HEADS, HEAD_DIM)
    v = v.reshape(B, S, KV_HEADS, HEAD_DIM)
    if cache is None:
        attn = context_attention(q, k, v, p['attn_sink'])
        h0_f = jnp.zeros((B, LRU_WIDTH), x.dtype)
        h0_b = jnp.zeros((B, LRU_WIDTH), x.dtype)
    else:
        k_ctx, v_ctx, h0_f, h0_b = cache
        attn = latent_attention(rope_2d(q), rope_2d(k), v, k_ctx, v_ctx, p['attn_sink'])
    rnn, hf_last, hb_last = rglru_branch(xr, p['conv_w'], p['conv_b'], p['lru_w_a'], p['lru_b_a'],
                                         p['lru_w_i'], p['lru_b_i'], p['lru_lambda'], h0_f, h0_b)
    a_out = attn @ p['w_attn_o']
    r_out = (rnn * jax.nn.gelu(xg)) @ p['w_rnn_o']
    mix = (jax.nn.sigmoid(ga) * a_out + jax.nn.sigmoid(gr) * r_out) @ p['w_out']
    x = x + g_a * mix
    h = rmsnorm(x, p['norm_ffn_g']) * (1.0 + sc_f) + sh_f
    x = x + g_f * peer(h, p['peer_w_q'], p['peer_keys'], p['peer_u'], p['peer_v'])
    return x, k, v, hf_last, hb_last


def setup_inputs(seed: int = 0) -> dict:
    key = jax.random.key(seed)
    ks = jax.random.split(key, 32)

    def nrm(k, shape, s):
        return s * jax.random.normal(k, shape, jnp.float32)

    u = jax.random.uniform(ks[20], (DEPTH, 2, LRU_WIDTH), jnp.float32, 0.9, 0.999)
    sig = u ** (1.0 / LRU_C)
    lru_lambda = jnp.log(sig) - jnp.log1p(-sig)
    return {
        "x_prompt": nrm(ks[0], (BATCH, SEQ, D_MODEL), 1.0),
        "x_sample": nrm(ks[1], (DEC_BATCH, DEC_SEQ, D_MODEL), 1.0),
        "c": nrm(ks[2], (DEC_BATCH, D_MODEL), 1.0),
        "cache_k": nrm(ks[3], (DEC_BATCH, DEPTH, PAST_LEN, KV_HEADS, HEAD_DIM), 1.0),
        "cache_v": nrm(ks[4], (DEC_BATCH, DEPTH, PAST_LEN, KV_HEADS, HEAD_DIM), 1.0),
        "state_h_fwd": nrm(ks[5], (DEC_BATCH, DEPTH, LRU_WIDTH), 0.5),
        "state_h_bwd": nrm(ks[6], (DEC_BATCH, DEPTH, LRU_WIDTH), 0.5),
        "c_ctx": nrm(ks[7], (D_MODEL,), 1.0),
        "norm_attn_g": 1.0 + nrm(ks[8], (DEPTH, D_MODEL), 0.05),
        "norm_ffn_g": 1.0 + nrm(ks[9], (DEPTH, D_MODEL), 0.05),
        "final_g": 1.0 + nrm(ks[10], (D_MODEL,), 0.05),
        "w_ada": nrm(ks[11], (DEPTH, D_MODEL, 6 * D_MODEL), D_MODEL ** -0.5),
        "b_ada": nrm(ks[12], (DEPTH, 6 * D_MODEL), 0.02),
        "w_in": nrm(ks[13], (DEPTH, D_MODEL, IN_COLS), D_MODEL ** -0.5),
        "attn_sink": nrm(ks[14], (DEPTH, N_HEADS), 0.5),
        "conv_w": nrm(ks[15], (DEPTH, CONV_W, LRU_WIDTH), CONV_W ** -0.5),
        "conv_b": nrm(ks[16], (DEPTH, LRU_WIDTH), 0.02),
        "lru_w_a": nrm(ks[17], (DEPTH, 2, LRU_BLOCKS, LRU_BLOCK_W, LRU_BLOCK_W), LRU_BLOCK_W ** -0.5),
        "lru_b_a": nrm(ks[18], (DEPTH, 2, LRU_WIDTH), 0.1),
        "lru_w_i": nrm(ks[19], (DEPTH, 2, LRU_BLOCKS, LRU_BLOCK_W, LRU_BLOCK_W), LRU_BLOCK_W ** -0.5),
        "lru_b_i": nrm(ks[21], (DEPTH, 2, LRU_WIDTH), 0.1),
        "lru_lambda": lru_lambda,
        "w_attn_o": nrm(ks[22], (DEPTH, ATTN_WIDTH, D_MODEL), ATTN_WIDTH ** -0.5),
        "w_rnn_o": nrm(ks[23], (DEPTH, LRU_WIDTH, D_MODEL), LRU_WIDTH ** -0.5),
        "w_out": nrm(ks[24], (DEPTH, D_MODEL, D_MODEL), D_MODEL ** -0.5),
        "peer_w_q": nrm(ks[25], (DEPTH, D_MODEL, PEER_HEADS * PK_DIM), D_MODEL ** -0.5),
        "peer_keys": nrm(ks[26], (DEPTH, PEER_HEADS, 2, N_KEYS, PK_HALF), PK_HALF ** -0.5),
        "peer_u": nrm(ks[27], (DEPTH, N_EXPERTS, D_MODEL), D_MODEL ** -0.5),
        "peer_v": nrm(ks[28], (DEPTH, N_EXPERTS, D_MODEL), 0.5),
    }


def reference(x_prompt, x_sample, c, cache_k, cache_v, state_h_fwd, state_h_bwd, c_ctx,
              norm_attn_g, norm_ffn_g, final_g, w_ada, b_ada, w_in, attn_sink, conv_w, conv_b,
              lru_w_a, lru_b_a, lru_w_i, lru_b_i, lru_lambda, w_attn_o, w_rnn_o, w_out,
              peer_w_q, peer_keys, peer_u, peer_v):
    xp, xs = x_prompt, x_sample
    ks_out, vs_out, hf_out, hb_out = [], [], [], []
    for l in range(DEPTH):
        p = {
            'norm_attn_g': norm_attn_g[l], 'norm_ffn_g': norm_ffn_g[l],
            'w_ada': w_ada[l], 'b_ada': b_ada[l], 'w_in': w_in[l], 'attn_sink': attn_sink[l],
            'conv_w': conv_w[l], 'conv_b': conv_b[l],
            'lru_w_a': lru_w_a[l], 'lru_b_a': lru_b_a[l], 'lru_w_i': lru_w_i[l], 'lru_b_i': lru_b_i[l],
            'lru_lambda': lru_lambda[l], 'w_attn_o': w_attn_o[l], 'w_rnn_o': w_rnn_o[l], 'w_out': w_out[l],
            'peer_w_q': peer_w_q[l], 'peer_keys': peer_keys[l], 'peer_u': peer_u[l], 'peer_v': peer_v[l],
        }
        xp, k_l, v_l, hf_l, hb_l = trunk_layer(xp, c_ctx[None, :], p, None)
        ks_out.append(k_l)
        vs_out.append(v_l)
        hf_out.append(hf_l)
        hb_out.append(hb_l)
        xs, _, _, _, _ = trunk_layer(xs, c, p, (cache_k[:, l], cache_v[:, l],
                                                state_h_fwd[:, l], state_h_bwd[:, l]))
    y_prompt = rmsnorm(xp, final_g)
    y_sample = rmsnorm(xs, final_g)
    new_cache_k = jnp.stack(ks_out, axis=1)
    new_cache_v = jnp.stack(vs_out, axis=1)
    new_state_h_fwd = jnp.stack(hf_out, axis=1)
    new_state_h_bwd = jnp.stack(hb_out, axis=1)
    return (y_prompt, y_sample, new_cache_k, new_cache_v, new_state_h_fwd, new_state_h_bwd)
```

```python
import functools

import jax
import jax.numpy as jnp
from jax import lax
from jax.experimental import pallas as pl
from jax.experimental.pallas import tpu as pltpu

D_MODEL = 1024
GRID_W = 64
N_HEADS = 8
KV_HEADS = 2
HEAD_DIM = 128
Q_GROUP = N_HEADS // KV_HEADS
ATTN_WIDTH = N_HEADS * HEAD_DIM
KV_WIDTH = KV_HEADS * HEAD_DIM
BLOCK = 128
ROPE_BASE = 10000.0
ROPE_AXIS_FREQS = HEAD_DIM // 4
LRU_WIDTH = 1024
LRU_BLOCKS = 8
LRU_BLOCK_W = LRU_WIDTH // LRU_BLOCKS
LRU_C = 8.0
CONV_W = 4
PEER_HEADS = 8
N_KEYS = 128
N_EXPERTS = N_KEYS * N_KEYS
PK_DIM = 256
PK_HALF = PK_DIM // 2
PEER_TOPK = 16
NORM_EPS = 1e-6
NEG_INF = -1e30

F32 = jnp.float32
BF16 = jnp.bfloat16

VMEM_LIMIT_BYTES = 56 * 1024 * 1024

_PAIR_COUNTS = {2: 5, 3: 4, 4: 3, 5: 2, 6: 2, 7: 2}


def _params(*sem):
    return pltpu.CompilerParams(dimension_semantics=sem, vmem_limit_bytes=VMEM_LIMIT_BYTES)


def _dot(a, b):
    return jnp.dot(a, b, preferred_element_type=F32)


def _dot_nt(a, b):
    return lax.dot_general(a, b, (((1,), (1,)), ((), ())), preferred_element_type=F32)


def _rms(x, g):
    return x * lax.rsqrt(jnp.mean(x * x, axis=-1, keepdims=True) + NORM_EPS) * g


def _ada_kernel(c_ref, w_ref, b_ref, o_ref):
    c = c_ref[...]
    s = c * jax.nn.sigmoid(c)
    w = w_ref[...]
    s_hi = s.astype(BF16)
    s_lo = (s - s_hi.astype(F32)).astype(BF16)
    w_hi = w.astype(BF16)
    w_lo = (w - w_hi.astype(F32)).astype(BF16)
    o_ref[...] = _dot(s_hi, w_hi) + (_dot(s_hi, w_lo) + _dot(s_lo, w_hi)) + b_ref[...]


def _ada(cond, w_ada, b_ada):
    rows = cond.shape[0]
    n = w_ada.shape[1]
    tn = 1536
    return pl.pallas_call(
        _ada_kernel,
        out_shape=jax.ShapeDtypeStruct((rows, n), F32),
        grid=(n // tn,),
        in_specs=[pl.BlockSpec((rows, D_MODEL), lambda j: (0, 0)),
                  pl.BlockSpec((D_MODEL, tn), lambda j: (0, j)),
                  pl.BlockSpec((1, tn), lambda j: (0, j))],
        out_specs=pl.BlockSpec((rows, tn), lambda j: (0, j)),
        compiler_params=_params("parallel"),
        name="adaln",
    )(cond, w_ada, b_ada)


_IN_WIDTHS = (ATTN_WIDTH, KV_WIDTH, KV_WIDTH, LRU_WIDTH, LRU_WIDTH, D_MODEL, D_MODEL)


def _inproj_kernel(x_ref, mod_ref, g_ref, w_ref, *out_refs):
    x = x_ref[...]
    sh = mod_ref[0, 0:1, :]
    sc = mod_ref[0, 1:2, :]
    h = (_rms(x, g_ref[...]) * (1.0 + sc) + sh).astype(BF16)
    off = 0
    for o_ref, width in zip(out_refs, _IN_WIDTHS):
        o_ref[...] = _dot(h, w_ref[:, off:off + width])
        off += width


def _inproj(x, mod, g, w_in_bf, tiles_per_mod_row, tm=256):
    t = x.shape[0]
    n = w_in_bf.shape[1]
    return pl.pallas_call(
        _inproj_kernel,
        out_shape=[jax.ShapeDtypeStruct((t, w), F32) for w in _IN_WIDTHS],
        grid=(t // tm,),
        in_specs=[pl.BlockSpec((tm, D_MODEL), lambda i: (i, 0)),
                  pl.BlockSpec((1, 6, D_MODEL), lambda i: (i // tiles_per_mod_row, 0, 0)),
                  pl.BlockSpec((1, D_MODEL), lambda i: (0, 0)),
                  pl.BlockSpec((D_MODEL, n), lambda i: (0, 0), pipeline_mode=pl.Buffered(1))],
        out_specs=[pl.BlockSpec((tm, w), lambda i: (i, 0)) for w in _IN_WIDTHS],
        compiler_params=_params("parallel"),
        name="inproj",
    )(x, mod, g, w_in_bf)


def _rope(x, c, s):
    return x * c + pltpu.roll(x, HEAD_DIM // 2, axis=1) * s


def _attn_kernel(*refs, window, n_blocks):
    if window:
        (q_ref, kw_ref, vw_ref, kc_ref, vc_ref, sink_ref, cq_ref, sq_ref, ck_ref, sk_ref, o_ref) = refs
    else:
        (q_ref, kc_ref, vc_ref, sink_ref, o_ref) = refs
    tq = q_ref.shape[0]
    n = pl.program_id(1)
    scale = HEAD_DIM ** -0.5
    rows = Q_GROUP * tq
    if window:
        r_in = lax.broadcasted_iota(jnp.int32, (rows, BLOCK), 0) % tq
        c_in = lax.broadcasted_iota(jnp.int32, (rows, BLOCK), 1)
        masks = {-1: (c_in >= r_in) & (n >= 1), 0: None, 1: (c_in <= r_in) & (n + 1 < n_blocks)}
        starts = {o: pl.multiple_of(jnp.clip(n + o, 0, n_blocks - 1) * BLOCK, BLOCK) for o in (-1, 0, 1)}
    head_of_row = lax.broadcasted_iota(jnp.int32, (rows, 1), 0) // tq
    for g in range(KV_HEADS):
        kcols = slice(g * HEAD_DIM, (g + 1) * HEAD_DIM)
        qs = []
        for h in range(Q_GROUP):
            qh = q_ref[:, (g * Q_GROUP + h) * HEAD_DIM:(g * Q_GROUP + h + 1) * HEAD_DIM]
            if window:
                qh = _rope(qh, cq_ref[...], sq_ref[...])
            qs.append(qh)
        qg = jnp.concatenate(qs, axis=0).astype(BF16)
        sink = jnp.zeros((rows, 1), F32)
        for h in range(Q_GROUP):
            sink = jnp.where(head_of_row == h, sink_ref[g * Q_GROUP + h], sink)
        scores, values = [], []
        if window:
            for o in (-1, 0, 1):
                kb = _rope(kw_ref[pl.ds(starts[o], BLOCK), kcols],
                           ck_ref[pl.ds(starts[o], BLOCK), :], sk_ref[pl.ds(starts[o], BLOCK), :])
                s = _dot_nt(qg, kb.astype(BF16)) * scale
                if masks[o] is not None:
                    s = jnp.where(masks[o], s, NEG_INF)
                scores.append(s)
                values.append(vw_ref[pl.ds(starts[o], BLOCK), kcols].astype(BF16))
        scores.append(_dot_nt(qg, kc_ref[:, kcols].astype(BF16)) * scale)
        values.append(vc_ref[:, kcols].astype(BF16))
        m = sink
        for s in scores:
            m = jnp.maximum(m, jnp.max(s, axis=-1, keepdims=True))
        denom = jnp.exp(sink - m)
        acc = jnp.zeros((rows, HEAD_DIM), F32)
        for s, v in zip(scores, values):
            p = jnp.exp(s - m)
            denom = denom + jnp.sum(p, axis=-1, keepdims=True)
            acc = acc + _dot(p.astype(BF16), v)
        out = acc / denom
        for h in range(Q_GROUP):
            o_ref[:, (g * Q_GROUP + h) * HEAD_DIM:(g * Q_GROUP + h + 1) * HEAD_DIM] = out[h * tq:(h + 1) * tq, :]


def _attention(q, k, v, k_ctx, v_ctx, sink, batch, seq, ctx_len, rope_tabs):
    window = k is not None
    tq = BLOCK
    nq = seq // tq
    q_spec = pl.BlockSpec((tq, ATTN_WIDTH), lambda b, n: (b * nq + n, 0))
    ctx_spec = pl.BlockSpec((ctx_len, KV_WIDTH), lambda b, n: (b, 0))
    sink_spec = pl.BlockSpec(memory_space=pltpu.SMEM)
    if window:
        own_spec = pl.BlockSpec((seq, KV_WIDTH), lambda b, n: (b, 0))
        tab_q = pl.BlockSpec((tq, HEAD_DIM), lambda b, n: (n, 0))
        tab_k = pl.BlockSpec((seq, HEAD_DIM), lambda b, n: (0, 0))
        cs, sn = rope_tabs
        in_specs = [q_spec, own_spec, own_spec, ctx_spec, ctx_spec, sink_spec, tab_q, tab_q, tab_k, tab_k]
        args = (q, k, v, k_ctx, v_ctx, sink, cs, sn, cs, sn)
    else:
        in_specs = [q_spec, ctx_spec, ctx_spec, sink_spec]
        args = (q, k_ctx, v_ctx, sink)
    return pl.pallas_call(
        functools.partial(_attn_kernel, window=window, n_blocks=nq),
        out_shape=jax.ShapeDtypeStruct((batch * seq, ATTN_WIDTH), F32),
        grid=(batch, nq),
        in_specs=in_specs,
        out_specs=q_spec,
        compiler_params=_params("parallel", "parallel"),
        name="attn_window" if window else "attn_ctx",
    )(*args)


def _rope_tables(seq):
    rows = seq // GRID_W
    row = jnp.repeat(jnp.arange(rows, dtype=F32), GRID_W)
    col = jnp.tile(jnp.arange(GRID_W, dtype=F32), rows)
    inv = ROPE_BASE ** (-jnp.arange(ROPE_AXIS_FREQS, dtype=F32) / ROPE_AXIS_FREQS)
    ang = jnp.concatenate([row[:, None] * inv, col[:, None] * inv], axis=-1)
    cos, sin = jnp.cos(ang), jnp.sin(ang)
    return jnp.concatenate([cos, cos], axis=-1), jnp.concatenate([-sin, sin], axis=-1)


def _rnn_kernel(xr_ref, xg_ref, cw_ref, cb_ref, w4_ref, b4_ref, lam_ref, h0f_ref, h0b_ref,
                o_ref, hf_ref, hb_ref, af, bf, ab, bb, *, batch, seq):
    w4 = w4_ref[0]
    b4 = b4_ref[...]
    lam = lam_ref[...]
    neg_c_sp = -LRU_C * jnp.logaddexp(-lam, 0.0)
    t_idx = lax.broadcasted_iota(jnp.int32, (seq, 1), 0)
    cw = cw_ref[...]
    cb = cb_ref[...]

    def gates(b, carry):
        rows = pl.ds(pl.multiple_of(b * seq, seq), seq)
        x = xr_ref[rows, :]
        xm2 = jnp.where(t_idx >= 2, pltpu.roll(x, 2, axis=0), 0.0)
        xm1 = jnp.where(t_idx >= 1, pltpu.roll(x, 1, axis=0), 0.0)
        xp1 = jnp.where(t_idx < seq - 1, pltpu.roll(x, seq - 1, axis=0), 0.0)
        xc = cb + xm2 * cw[0:1] + xm1 * cw[1:2] + x * cw[2:3] + xp1 * cw[3:4]
        z = _dot(xc.astype(BF16), w4) + b4
        for d, (a_ref, b_ref) in enumerate(((af, bf), (ab, bb))):
            r = jax.nn.sigmoid(z[:, (2 * d) * LRU_BLOCK_W:(2 * d + 1) * LRU_BLOCK_W])
            ig = jax.nn.sigmoid(z[:, (2 * d + 1) * LRU_BLOCK_W:(2 * d + 2) * LRU_BLOCK_W])
            log_a = r * neg_c_sp[d:d + 1]
            a = jnp.exp(log_a)
            a_ref[rows, :] = a
            b_ref[rows, :] = jnp.sqrt(jnp.tanh(-log_a) * (a * a + 1.0)) * ig * xc
        return carry

    lax.fori_loop(0, batch, gates, 0)

    def step(t, carry):
        hf, hb = carry
        rf = pl.ds(t, batch, stride=seq)
        hf = af[rf, :] * hf + bf[rf, :]
        bf[rf, :] = hf
        rb = pl.ds(seq - 1 - t, batch, stride=seq)
        hb = ab[rb, :] * hb + bb[rb, :]
        bb[rb, :] = hb
        return hf, hb

    hf, hb = lax.fori_loop(0, seq, step, (h0f_ref[...], h0b_ref[...]), unroll=8)
    hf_ref[...] = hf
    hb_ref[...] = hb

    def finish(b, carry):
        rows = pl.ds(pl.multiple_of(b * seq, seq), seq)
        o_ref[rows, :] = (bf[rows, :] + bb[rows, :]) * jax.nn.gelu(xg_ref[rows, :])
        return carry

    lax.fori_loop(0, batch, finish, 0)


def _rnn(xr, xg, conv_w, conv_b, w4, b4, lam, h0f, h0b, batch, seq):
    t = batch * seq
    cblk = LRU_BLOCK_W
    col = lambda n: (0, n)
    return pl.pallas_call(
        functools.partial(_rnn_kernel, batch=batch, seq=seq),
        out_shape=[jax.ShapeDtypeStruct((t, LRU_WIDTH), F32),
                   jax.ShapeDtypeStruct((batch, LRU_WIDTH), F32),
                   jax.ShapeDtypeStruct((batch, LRU_WIDTH), F32)],
        grid=(LRU_BLOCKS,),
        in_specs=[pl.BlockSpec((t, cblk), col),
                  pl.BlockSpec((t, cblk), col),
                  pl.BlockSpec((CONV_W, cblk), col),
                  pl.BlockSpec((1, cblk), col),
                  pl.BlockSpec((1, cblk, 4 * cblk), lambda n: (n, 0, 0)),
                  pl.BlockSpec((1, 4 * cblk), col),
                  pl.BlockSpec((2, cblk), col),
                  pl.BlockSpec((batch, cblk), col),
                  pl.BlockSpec((batch, cblk), col)],
        out_specs=[pl.BlockSpec((t, cblk), col),
                   pl.BlockSpec((batch, cblk), col),
                   pl.BlockSpec((batch, cblk), col)],
        scratch_shapes=[pltpu.VMEM((t, cblk), F32) for _ in range(4)],
        compiler_params=_params("parallel"),
        name="rglru",
    )(xr, xg, conv_w, conv_b, w4, b4, lam, h0f, h0b)


def _outproj_kernel(x_ref, attn_ref, rnn_ref, ga_ref, gr_ref, mod_ref, g_ref,
                    wa_ref, wr_ref, wo_ref, x1_ref, h2_ref):
    a_out = _dot(attn_ref[...].astype(BF16), wa_ref[...])
    r_out = _dot(rnn_ref[...].astype(BF16), wr_ref[...])
    mix_in = jax.nn.sigmoid(ga_ref[...]) * a_out + jax.nn.sigmoid(gr_ref[...]) * r_out
    mix = _dot(mix_in.astype(BF16), wo_ref[...])
    x1 = x_ref[...] + mod_ref[0, 2:3, :] * mix
    x1_ref[...] = x1
    h2 = _rms(x1, g_ref[...]) * (1.0 + mod_ref[0, 4:5, :]) + mod_ref[0, 3:4, :]
    h2_ref[...] = h2.astype(BF16)


def _outproj(x, attn, rnn, ga, gr, mod, g, wa, wr, wo, tiles_per_mod_row, tm=256):
    t = x.shape[0]
    tok = pl.BlockSpec((tm, D_MODEL), lambda i: (i, 0))
    wspec = pl.BlockSpec((D_MODEL, D_MODEL), lambda i: (0, 0))
    return pl.pallas_call(
        _outproj_kernel,
        out_shape=[jax.ShapeDtypeStruct((t, D_MODEL), F32), jax.ShapeDtypeStruct((t, D_MODEL), BF16)],
        grid=(t // tm,),
        in_specs=[tok, tok, tok, tok, tok,
                  pl.BlockSpec((1, 6, D_MODEL), lambda i: (i // tiles_per_mod_row, 0, 0)),
                  pl.BlockSpec((1, D_MODEL), lambda i: (0, 0)),
                  wspec, wspec, wspec],
        out_specs=[tok, tok],
        compiler_params=_params("parallel"),
        name="outproj",
    )(x, attn, rnn, ga, gr, mod, g, wa, wr, wo)


def _top16_rows(s):
    n, t = s.shape
    key = lax.broadcasted_iota(jnp.int32, (n, t), 0).astype(F32)
    slot = lax.broadcasted_iota(jnp.int32, (PEER_TOPK, t), 0)
    rank = jnp.full((n, t), float(PEER_TOPK), F32)
    top = jnp.zeros((PEER_TOPK, t), F32)
    for k in range(PEER_TOPK):
        m = jnp.max(s, axis=0, keepdims=True)
        first = jnp.min(jnp.where(s == m, key, float(n)), axis=0, keepdims=True)
        hit = key == first
        s = jnp.where(hit, -jnp.inf, s)
        rank = jnp.where(hit, float(k), rank)
        top = jnp.where(slot == k, m, top)
    return top, rank


def _pair_rows(v0, v1, op, pad):
    t = v0.shape[1]
    b_idx = lax.broadcasted_iota(jnp.int32, (8, t), 0)
    groups = [op(v0[0:1], v1), op(v0[1:2], v1[0:8])]
    for a in range(2, 8):
        groups.append(jnp.where(b_idx < _PAIR_COUNTS[a], op(v0[a:a + 1], v1[0:8]), pad))
    groups.append(op(v0[8:16], v1[0:1]))
    return jnp.concatenate(groups, axis=0)


def _peer_topk_kernel(h2_ref, wq_ref, keys_ref, a_ref, nb_ref, bv_ref, r1_ref):
    q = _dot(h2_ref[...], wq_ref[...])
    s0 = _dot_nt(keys_ref[0, 0], q[:, :PK_HALF].astype(BF16))
    s1 = _dot_nt(keys_ref[0, 1], q[:, PK_HALF:].astype(BF16))
    top0, rank0 = _top16_rows(s0)
    top1, rank1 = _top16_rows(s1)
    t = s0.shape[1]

    cand0 = _pair_rows(top0, top1, jnp.add, -jnp.inf)
    n_c = cand0.shape[0]
    pos = lax.broadcasted_iota(jnp.int32, (n_c, t), 0).astype(F32)
    cand = cand0
    for _ in range(PEER_TOPK):
        m = jnp.max(cand, axis=0, keepdims=True)
        first = jnp.min(jnp.where(cand == m, pos, float(n_c)), axis=0, keepdims=True)
        cand = jnp.where(pos == first, -jnp.inf, cand)
    sel = cand != cand0

    e0 = jnp.exp(top0 - top0[0:1])
    e1 = jnp.exp(top1 - top1[0:1])
    pair_e = _pair_rows(e0, e1, jnp.multiply, 0.0)
    z = jnp.sum(jnp.where(sel, pair_e, 0.0), axis=0, keepdims=True)
    inv_z = 1.0 / z

    self_f = jnp.where(sel, 1.0, 0.0)
    row8 = lax.broadcasted_iota(jnp.int32, (8, t), 0)
    counts = jnp.zeros((8, t), F32)
    bounds = [(0, 16)] + [(16 + 8 * (a - 1), 24 + 8 * (a - 1)) for a in range(1, 8)]
    for a, (lo, hi) in enumerate(bounds):
        counts = jnp.where(row8 == a, jnp.sum(self_f[lo:hi], axis=0, keepdims=True), counts)
    nb16 = jnp.concatenate([counts, self_f[72:80]], axis=0)

    nb = jnp.zeros_like(rank0)
    for a in range(PEER_TOPK):
        nb = jnp.where(rank0 == float(a), nb16[a:a + 1], nb)
    a_ref[0] = jnp.exp(s0 - top0[0:1]) * inv_z
    nb_ref[0] = nb
    bv_ref[0] = jnp.exp(s1 - top1[0:1])
    r1_ref[0] = rank1


def _peer_topk(h2, wq_bf, keys_bf, tt=256):
    t = h2.shape[0]
    out = jax.ShapeDtypeStruct((PEER_HEADS, N_KEYS, t), F32)
    ospec = pl.BlockSpec((1, N_KEYS, tt), lambda i, h: (h, 0, i))
    return pl.pallas_call(
        _peer_topk_kernel,
        out_shape=[out, out, out, out],
        grid=(t // tt, PEER_HEADS),
        in_specs=[pl.BlockSpec((tt, D_MODEL), lambda i, h: (i, 0)),
                  pl.BlockSpec((D_MODEL, PK_DIM), lambda i, h: (0, h)),
                  pl.BlockSpec((1, 2, N_KEYS, PK_HALF), lambda i, h: (h, 0, 0, 0))],
        out_specs=[ospec, ospec, ospec, ospec],
        compiler_params=_params("parallel", "parallel"),
        name="peer_topk",
    )(h2, wq_bf, keys_bf)


def _peer_mix_kernel(h2_ref, u_ref, vt_ref, a_ref, nb_ref, bv_ref, r1_ref, x1_ref, mod_ref, g_ref,
                     y_ref, acc_ref, *, key_rows):
    e = pl.program_id(1)

    @pl.when(e == 0)
    def _():
        acc_ref[...] = jnp.zeros_like(acc_ref)

    act = jax.nn.gelu(_dot_nt(u_ref[...], h2_ref[...]))
    weights = []
    for r in range(key_rows):
        i = e * key_rows + r
        gate = jnp.zeros((N_KEYS, act.shape[1]), F32)
        for h in range(PEER_HEADS):
            a_row = a_ref[h, pl.ds(i, 1), :]
            nb_row = nb_ref[h, pl.ds(i, 1), :]
            gate = gate + a_row * jnp.where(r1_ref[h] < nb_row, bv_ref[h], 0.0)
        weights.append((gate * act[r * N_KEYS:(r + 1) * N_KEYS]).astype(BF16))
    w = jnp.concatenate(weights, axis=0)
    acc_ref[...] += _dot(vt_ref[...], w)

    @pl.when(e == pl.num_programs(1) - 1)
    def _():
        x2 = x1_ref[...] + mod_ref[0, 5:6, :] * acc_ref[...].T
        y_ref[...] = _rms(x2, g_ref[...])


def _peer_mix(h2, u_bf, vt_bf, a, nb, bv, r1, x1, mod, final_g, tiles_per_mod_row, tt=512, key_rows=4):
    t = h2.shape[0]
    et = key_rows * N_KEYS
    tok = pl.BlockSpec((tt, D_MODEL), lambda i, e: (i, 0))
    sel = pl.BlockSpec((PEER_HEADS, N_KEYS, tt), lambda i, e: (0, 0, i))
    return pl.pallas_call(
        functools.partial(_peer_mix_kernel, key_rows=key_rows),
        out_shape=jax.ShapeDtypeStruct((t, D_MODEL), F32),
        grid=(t // tt, N_EXPERTS // et),
        in_specs=[tok,
                  pl.BlockSpec((et, D_MODEL), lambda i, e: (e, 0)),
                  pl.BlockSpec((D_MODEL, et), lambda i, e: (0, e)),
                  sel, sel, sel, sel,
                  tok,
                  pl.BlockSpec((1, 6, D_MODEL), lambda i, e: (i // tiles_per_mod_row, 0, 0)),
                  pl.BlockSpec((1, D_MODEL), lambda i, e: (0, 0))],
        out_specs=tok,
        scratch_shapes=[pltpu.VMEM((D_MODEL, tt), F32)],
        compiler_params=_params("parallel", "arbitrary"),
        name="peer_mix",
    )(h2, u_bf, vt_bf, a, nb, bv, r1, x1, mod, final_g)


def _tiles_per_mod_row(n_mod_rows, n_tokens, seq, tile):
    return seq // tile if n_mod_rows > 1 else n_tokens // tile


def _trunk(x, mod, w, batch, seq, cache):
    t = batch * seq
    n_mod = mod.shape[0]
    q, k, v, xr, xg, ga, gr = _inproj(x, mod, w["norm_attn_g"], w["w_in"],
                                      _tiles_per_mod_row(n_mod, t, seq, 256))
    if cache is None:
        attn = _attention(q, None, None, k, v, w["attn_sink"], batch, seq, seq, None)
        h0f = jnp.zeros((batch, LRU_WIDTH), F32)
        h0b = jnp.zeros((batch, LRU_WIDTH), F32)
    else:
        k_ctx, v_ctx, h0f, h0b = cache
        attn = _attention(q, k, v, k_ctx, v_ctx, w["attn_sink"], batch, seq,
                          k_ctx.shape[0] // batch, _rope_tables(seq))
    rnn, hf_last, hb_last = _rnn(xr, xg, w["conv_w"], w["conv_b"], w["lru_w4"], w["lru_b4"],
                                 w["lru_lambda"], h0f, h0b, batch, seq)
    x1, h2 = _outproj(x, attn, rnn, ga, gr, mod, w["norm_ffn_g"], w["w_attn_o"], w["w_rnn_o"],
                      w["w_out"], _tiles_per_mod_row(n_mod, t, seq, 256))
    a, nb, bv, r1 = _peer_topk(h2, w["peer_w_q"], w["peer_keys"])
    y = _peer_mix(h2, w["peer_u"], w["peer_vt"], a, nb, bv, r1, x1, mod, w["final_g"],
                  _tiles_per_mod_row(n_mod, t, seq, 512))
    return y, k, v, hf_last, hb_last


def kernel(x_prompt, x_sample, c, cache_k, cache_v, state_h_fwd, state_h_bwd, c_ctx, norm_attn_g, norm_ffn_g, final_g, w_ada, b_ada, w_in, attn_sink, conv_w, conv_b, lru_w_a, lru_b_a, lru_w_i, lru_b_i, lru_lambda, w_attn_o, w_rnn_o, w_out, peer_w_q, peer_keys, peer_u, peer_v):
    depth = w_in.shape[0]
    assert depth == 1, "single trunk layer"
    batch, seq, _ = x_prompt.shape
    dec_batch, dec_seq, _ = x_sample.shape
    past_len = cache_k.shape[2]
    l = 0

    cond = jnp.concatenate([jnp.broadcast_to(c_ctx[None, :], (8, D_MODEL)), c], axis=0)
    mod = _ada(cond, w_ada[l], b_ada[l][None, :]).reshape(-1, 6, D_MODEL)
    mod_ctx, mod_lat = mod[0:1], mod[8:]

    lru_w4 = jnp.concatenate([lru_w_a[l, 0], lru_w_i[l, 0], lru_w_a[l, 1], lru_w_i[l, 1]], axis=-1).astype(BF16)
    lru_b4 = jnp.stack([lru_b_a[l, 0], lru_b_i[l, 0], lru_b_a[l, 1], lru_b_i[l, 1]], axis=0)
    lru_b4 = lru_b4.reshape(4, LRU_BLOCKS, LRU_BLOCK_W).transpose(1, 0, 2).reshape(1, 4 * LRU_WIDTH)

    w = {
        "norm_attn_g": norm_attn_g[l][None, :], "norm_ffn_g": norm_ffn_g[l][None, :], "final_g": final_g[None, :],
        "w_in": w_in[l].astype(BF16), "attn_sink": attn_sink[l],
        "conv_w": conv_w[l], "conv_b": conv_b[l][None, :],
        "lru_w4": lru_w4, "lru_b4": lru_b4, "lru_lambda": lru_lambda[l],
        "w_attn_o": w_attn_o[l].astype(BF16), "w_rnn_o": w_rnn_o[l].astype(BF16), "w_out": w_out[l].astype(BF16),
        "peer_w_q": peer_w_q[l].astype(BF16), "peer_keys": peer_keys[l].astype(BF16),
        "peer_u": peer_u[l].astype(BF16), "peer_vt": peer_v[l].T.astype(BF16),
    }

    yp, k_l, v_l, hf_l, hb_l = _trunk(x_prompt.reshape(batch * seq, D_MODEL), mod_ctx, w, batch, seq, None)
    cache = (cache_k[:, l].reshape(dec_batch * past_len, KV_WIDTH),
             cache_v[:, l].reshape(dec_batch * past_len, KV_WIDTH),
             state_h_fwd[:, l], state_h_bwd[:, l])
    ys, _, _, _, _ = _trunk(x_sample.reshape(dec_batch * dec_seq, D_MODEL), mod_lat, w, dec_batch, dec_seq, cache)

    return (yp.reshape(batch, seq, D_MODEL),
            ys.reshape(dec_batch, dec_seq, D_MODEL),
            k_l.reshape(batch, 1, seq, KV_HEADS, HEAD_DIM),
            v_l.reshape(batch, 1, seq, KV_HEADS, HEAD_DIM),
            hf_l[:, None, :],
            hb_l[:, None, :])
```

```python
import functools

import jax
import jax.numpy as jnp
from jax import lax
from jax.experimental import pallas as pl
from jax.experimental.pallas import tpu as pltpu

D_MODEL = 1024
GRID_W = 64
N_HEADS = 8
KV_HEADS = 2
HEAD_DIM = 128
Q_GROUP = N_HEADS // KV_HEADS
ATTN_WIDTH = N_HEADS * HEAD_DIM
KV_WIDTH = KV_HEADS * HEAD_DIM
BLOCK = 128
ROPE_BASE = 10000.0
ROPE_AXIS_FREQS = HEAD_DIM // 4
LRU_WIDTH = 1024
LRU_BLOCKS = 8
LRU_BLOCK_W = LRU_WIDTH // LRU_BLOCKS
LRU_C = 8.0
CONV_W = 4
PEER_HEADS = 8
N_KEYS = 128
N_EXPERTS = N_KEYS * N_KEYS
PK_DIM = 256
PK_HALF = PK_DIM // 2
PEER_TOPK = 16
NORM_EPS = 1e-6
NEG_INF = -1e30

F32 = jnp.float32
BF16 = jnp.bfloat16

VMEM_LIMIT_BYTES = 56 * 1024 * 1024

_PAIR_COUNTS = {2: 5, 3: 4, 4: 3, 5: 2, 6: 2, 7: 2}

MXU_DIM = 256
GATE_LANES = 256


def _params(*sem):
    return pltpu.CompilerParams(dimension_semantics=sem, vmem_limit_bytes=VMEM_LIMIT_BYTES)


def _dot(a, b):
    return jnp.dot(a, b, preferred_element_type=F32)


def _dot_nt(a, b):
    return lax.dot_general(a, b, (((1,), (1,)), ((), ())), preferred_element_type=F32)


def _rms(x, g):
    return x * lax.rsqrt(jnp.mean(x * x, axis=-1, keepdims=True) + NORM_EPS) * g


def _ada_kernel(c_ref, w_ref, b_ref, o_ref):
    c = c_ref[...]
    s = c * jax.nn.sigmoid(c)
    w = w_ref[...]
    s_hi = s.astype(BF16)
    s_lo = (s - s_hi.astype(F32)).astype(BF16)
    w_hi = w.astype(BF16)
    w_lo = (w - w_hi.astype(F32)).astype(BF16)
    o_ref[...] = _dot(s_hi, w_hi) + (_dot(s_hi, w_lo) + _dot(s_lo, w_hi)) + b_ref[...]


def _ada(cond, w_ada, b_ada):
    rows = cond.shape[0]
    n = w_ada.shape[1]
    tn = 1536
    return pl.pallas_call(
        _ada_kernel,
        out_shape=jax.ShapeDtypeStruct((rows, n), F32),
        grid=(n // tn,),
        in_specs=[pl.BlockSpec((rows, D_MODEL), lambda j: (0, 0)),
                  pl.BlockSpec((D_MODEL, tn), lambda j: (0, j)),
                  pl.BlockSpec((1, tn), lambda j: (0, j))],
        out_specs=pl.BlockSpec((rows, tn), lambda j: (0, j)),
        compiler_params=_params("parallel"),
        name="adaln",
    )(cond, w_ada, b_ada)


_IN_WIDTHS = (ATTN_WIDTH, KV_WIDTH, KV_WIDTH, LRU_WIDTH, LRU_WIDTH, D_MODEL, D_MODEL)


def _inproj_kernel(x_ref, mod_ref, g_ref, w_ref, *out_refs):
    x = x_ref[...]
    sh = mod_ref[0, 0:1, :]
    sc = mod_ref[0, 1:2, :]
    h = (_rms(x, g_ref[...]) * (1.0 + sc) + sh).astype(BF16)
    off = 0
    for o_ref, width in zip(out_refs, _IN_WIDTHS):
        o_ref[...] = _dot(h, w_ref[:, off:off + width])
        off += width


def _inproj(x, mod, g, w_in_bf, tiles_per_mod_row, tm=256):
    t = x.shape[0]
    n = w_in_bf.shape[1]
    return pl.pallas_call(
        _inproj_kernel,
        out_shape=[jax.ShapeDtypeStruct((t, w), F32) for w in _IN_WIDTHS],
        grid=(t // tm,),
        in_specs=[pl.BlockSpec((tm, D_MODEL), lambda i: (i, 0)),
                  pl.BlockSpec((1, 6, D_MODEL), lambda i: (i // tiles_per_mod_row, 0, 0)),
                  pl.BlockSpec((1, D_MODEL), lambda i: (0, 0)),
                  pl.BlockSpec((D_MODEL, n), lambda i: (0, 0), pipeline_mode=pl.Buffered(1))],
        out_specs=[pl.BlockSpec((tm, w), lambda i: (i, 0)) for w in _IN_WIDTHS],
        compiler_params=_params("parallel"),
        name="inproj",
    )(x, mod, g, w_in_bf)


def _rope(x, c, s):
    return x * c + pltpu.roll(x, HEAD_DIM // 2, axis=1) * s


def _attn_kernel(*refs, window, n_blocks):
    if window:
        (q_ref, kw_ref, vw_ref, kc_ref, vc_ref, sink_ref, cq_ref, sq_ref, ck_ref, sk_ref, o_ref) = refs
    else:
        (q_ref, kc_ref, vc_ref, sink_ref, o_ref) = refs
    tq = q_ref.shape[0]
    n = pl.program_id(1)
    scale = HEAD_DIM ** -0.5
    rows = Q_GROUP * tq
    if window:
        r_in = lax.broadcasted_iota(jnp.int32, (rows, BLOCK), 0) % tq
        c_in = lax.broadcasted_iota(jnp.int32, (rows, BLOCK), 1)
        masks = {-1: (c_in >= r_in) & (n >= 1), 0: None, 1: (c_in <= r_in) & (n + 1 < n_blocks)}
        starts = {o: pl.multiple_of(jnp.clip(n + o, 0, n_blocks - 1) * BLOCK, BLOCK) for o in (-1, 0, 1)}
    head_of_row = lax.broadcasted_iota(jnp.int32, (rows, 1), 0) // tq
    for g in range(KV_HEADS):
        kcols = slice(g * HEAD_DIM, (g + 1) * HEAD_DIM)
        qs = []
        for h in range(Q_GROUP):
            qh = q_ref[:, (g * Q_GROUP + h) * HEAD_DIM:(g * Q_GROUP + h + 1) * HEAD_DIM]
            if window:
                qh = _rope(qh, cq_ref[...], sq_ref[...])
            qs.append(qh)
        qg = jnp.concatenate(qs, axis=0).astype(BF16)
        sink = jnp.zeros((rows, 1), F32)
        for h in range(Q_GROUP):
            sink = jnp.where(head_of_row == h, sink_ref[g * Q_GROUP + h], sink)
        scores, values = [], []
        if window:
            for o in (-1, 0, 1):
                kb = _rope(kw_ref[pl.ds(starts[o], BLOCK), kcols],
                           ck_ref[pl.ds(starts[o], BLOCK), :], sk_ref[pl.ds(starts[o], BLOCK), :])
                s = _dot_nt(qg, kb.astype(BF16)) * scale
                if masks[o] is not None:
                    s = jnp.where(masks[o], s, NEG_INF)
                scores.append(s)
                values.append(vw_ref[pl.ds(starts[o], BLOCK), kcols].astype(BF16))
        scores.append(_dot_nt(qg, kc_ref[:, kcols].astype(BF16)) * scale)
        values.append(vc_ref[:, kcols].astype(BF16))
        m = sink
        for s in scores:
            m = jnp.maximum(m, jnp.max(s, axis=-1, keepdims=True))
        denom = jnp.exp(sink - m)
        acc = jnp.zeros((rows, HEAD_DIM), F32)
        for s, v in zip(scores, values):
            p = jnp.exp(s - m)
            denom = denom + jnp.sum(p, axis=-1, keepdims=True)
            acc = acc + _dot(p.astype(BF16), v)
        out = acc / denom
        for h in range(Q_GROUP):
            o_ref[:, (g * Q_GROUP + h) * HEAD_DIM:(g * Q_GROUP + h + 1) * HEAD_DIM] = out[h * tq:(h + 1) * tq, :]


def _attention(q, k, v, k_ctx, v_ctx, sink, batch, seq, ctx_len, rope_tabs):
    window = k is not None
    tq = BLOCK
    nq = seq // tq
    q_spec = pl.BlockSpec((tq, ATTN_WIDTH), lambda b, n: (b * nq + n, 0))
    ctx_spec = pl.BlockSpec((ctx_len, KV_WIDTH), lambda b, n: (b, 0))
    sink_spec = pl.BlockSpec(memory_space=pltpu.SMEM)
    if window:
        own_spec = pl.BlockSpec((seq, KV_WIDTH), lambda b, n: (b, 0))
        tab_q = pl.BlockSpec((tq, HEAD_DIM), lambda b, n: (n, 0))
        tab_k = pl.BlockSpec((seq, HEAD_DIM), lambda b, n: (0, 0))
        cs, sn = rope_tabs
        in_specs = [q_spec, own_spec, own_spec, ctx_spec, ctx_spec, sink_spec, tab_q, tab_q, tab_k, tab_k]
        args = (q, k, v, k_ctx, v_ctx, sink, cs, sn, cs, sn)
    else:
        in_specs = [q_spec, ctx_spec, ctx_spec, sink_spec]
        args = (q, k_ctx, v_ctx, sink)
    return pl.pallas_call(
        functools.partial(_attn_kernel, window=window, n_blocks=nq),
        out_shape=jax.ShapeDtypeStruct((batch * seq, ATTN_WIDTH), F32),
        grid=(batch, nq),
        in_specs=in_specs,
        out_specs=q_spec,
        compiler_params=_params("parallel", "parallel"),
        name="attn_window" if window else "attn_ctx",
    )(*args)


def _rope_tables(seq):
    rows = seq // GRID_W
    row = jnp.repeat(jnp.arange(rows, dtype=F32), GRID_W)
    col = jnp.tile(jnp.arange(GRID_W, dtype=F32), rows)
    inv = ROPE_BASE ** (-jnp.arange(ROPE_AXIS_FREQS, dtype=F32) / ROPE_AXIS_FREQS)
    ang = jnp.concatenate([row[:, None] * inv, col[:, None] * inv], axis=-1)
    cos, sin = jnp.cos(ang), jnp.sin(ang)
    return jnp.concatenate([cos, cos], axis=-1), jnp.concatenate([-sin, sin], axis=-1)


def _rnn_kernel(xr_ref, xg_ref, cw_ref, cb_ref, w4_ref, b4_ref, lam_ref, h0f_ref, h0b_ref,
                o_ref, hf_ref, hb_ref, af, bf, ab, bb, *, batch, seq):
    w4 = w4_ref[0]
    b4 = b4_ref[...]
    lam = lam_ref[...]
    neg_c_sp = -LRU_C * jnp.logaddexp(-lam, 0.0)
    t_idx = lax.broadcasted_iota(jnp.int32, (seq, 1), 0)
    cw = cw_ref[...]
    cb = cb_ref[...]

    def gates(b, carry):
        rows = pl.ds(pl.multiple_of(b * seq, seq), seq)
        x = xr_ref[rows, :]
        xm2 = jnp.where(t_idx >= 2, pltpu.roll(x, 2, axis=0), 0.0)
        xm1 = jnp.where(t_idx >= 1, pltpu.roll(x, 1, axis=0), 0.0)
        xp1 = jnp.where(t_idx < seq - 1, pltpu.roll(x, seq - 1, axis=0), 0.0)
        xc = cb + xm2 * cw[0:1] + xm1 * cw[1:2] + x * cw[2:3] + xp1 * cw[3:4]
        z = _dot(xc.astype(BF16), w4) + b4
        for d, (a_ref, b_ref) in enumerate(((af, bf), (ab, bb))):
            r = jax.nn.sigmoid(z[:, (2 * d) * LRU_BLOCK_W:(2 * d + 1) * LRU_BLOCK_W])
            ig = jax.nn.sigmoid(z[:, (2 * d + 1) * LRU_BLOCK_W:(2 * d + 2) * LRU_BLOCK_W])
            log_a = r * neg_c_sp[d:d + 1]
            a = jnp.exp(log_a)
            a_ref[rows, :] = a
            b_ref[rows, :] = jnp.sqrt(jnp.tanh(-log_a) * (a * a + 1.0)) * ig * xc
        return carry

    lax.fori_loop(0, batch, gates, 0)

    def step(t, carry):
        hf, hb = carry
        rf = pl.ds(t, batch, stride=seq)
        hf = af[rf, :] * hf + bf[rf, :]
        bf[rf, :] = hf
        rb = pl.ds(seq - 1 - t, batch, stride=seq)
        hb = ab[rb, :] * hb + bb[rb, :]
        bb[rb, :] = hb
        return hf, hb

    hf, hb = lax.fori_loop(0, seq, step, (h0f_ref[...], h0b_ref[...]), unroll=8)
    hf_ref[...] = hf
    hb_ref[...] = hb

    def finish(b, carry):
        rows = pl.ds(pl.multiple_of(b * seq, seq), seq)
        o_ref[rows, :] = (bf[rows, :] + bb[rows, :]) * jax.nn.gelu(xg_ref[rows, :])
        return carry

    lax.fori_loop(0, batch, finish, 0)


def _rnn(xr, xg, conv_w, conv_b, w4, b4, lam, h0f, h0b, batch, seq):
    t = batch * seq
    cblk = LRU_BLOCK_W
    col = lambda n: (0, n)
    return pl.pallas_call(
        functools.partial(_rnn_kernel, batch=batch, seq=seq),
        out_shape=[jax.ShapeDtypeStruct((t, LRU_WIDTH), F32),
                   jax.ShapeDtypeStruct((batch, LRU_WIDTH), F32),
                   jax.ShapeDtypeStruct((batch, LRU_WIDTH), F32)],
        grid=(LRU_BLOCKS,),
        in_specs=[pl.BlockSpec((t, cblk), col),
                  pl.BlockSpec((t, cblk), col),
                  pl.BlockSpec((CONV_W, cblk), col),
                  pl.BlockSpec((1, cblk), col),
                  pl.BlockSpec((1, cblk, 4 * cblk), lambda n: (n, 0, 0)),
                  pl.BlockSpec((1, 4 * cblk), col),
                  pl.BlockSpec((2, cblk), col),
                  pl.BlockSpec((batch, cblk), col),
                  pl.BlockSpec((batch, cblk), col)],
        out_specs=[pl.BlockSpec((t, cblk), col),
                   pl.BlockSpec((batch, cblk), col),
                   pl.BlockSpec((batch, cblk), col)],
        scratch_shapes=[pltpu.VMEM((t, cblk), F32) for _ in range(4)],
        compiler_params=_params("parallel"),
        name="rglru",
    )(xr, xg, conv_w, conv_b, w4, b4, lam, h0f, h0b)


def _outproj_kernel(x_ref, attn_ref, rnn_ref, ga_ref, gr_ref, mod_ref, g_ref,
                    wa_ref, wr_ref, wo_ref, x1_ref, h2_ref):
    a_out = _dot(attn_ref[...].astype(BF16), wa_ref[...])
    r_out = _dot(rnn_ref[...].astype(BF16), wr_ref[...])
    mix_in = jax.nn.sigmoid(ga_ref[...]) * a_out + jax.nn.sigmoid(gr_ref[...]) * r_out
    mix = _dot(mix_in.astype(BF16), wo_ref[...])
    x1 = x_ref[...] + mod_ref[0, 2:3, :] * mix
    x1_ref[...] = x1
    h2 = _rms(x1, g_ref[...]) * (1.0 + mod_ref[0, 4:5, :]) + mod_ref[0, 3:4, :]
    h2_ref[...] = h2.astype(BF16)


def _outproj(x, attn, rnn, ga, gr, mod, g, wa, wr, wo, tiles_per_mod_row, tm=256):
    t = x.shape[0]
    tok = pl.BlockSpec((tm, D_MODEL), lambda i: (i, 0))
    wspec = pl.BlockSpec((D_MODEL, D_MODEL), lambda i: (0, 0))
    return pl.pallas_call(
        _outproj_kernel,
        out_shape=[jax.ShapeDtypeStruct((t, D_MODEL), F32), jax.ShapeDtypeStruct((t, D_MODEL), BF16)],
        grid=(t // tm,),
        in_specs=[tok, tok, tok, tok, tok,
                  pl.BlockSpec((1, 6, D_MODEL), lambda i: (i // tiles_per_mod_row, 0, 0)),
                  pl.BlockSpec((1, D_MODEL), lambda i: (0, 0)),
                  wspec, wspec, wspec],
        out_specs=[tok, tok],
        compiler_params=_params("parallel"),
        name="outproj",
    )(x, attn, rnn, ga, gr, mod, g, wa, wr, wo)


def _top16_rows(s):
    n, t = s.shape
    key = lax.broadcasted_iota(jnp.int32, (n, t), 0).astype(F32)
    slot = lax.broadcasted_iota(jnp.int32, (PEER_TOPK, t), 0)
    rank = jnp.full((n, t), float(PEER_TOPK), F32)
    top = jnp.zeros((PEER_TOPK, t), F32)
    for k in range(PEER_TOPK):
        m = jnp.max(s, axis=0, keepdims=True)
        first = jnp.min(jnp.where(s == m, key, float(n)), axis=0, keepdims=True)
        hit = key == first
        s = jnp.where(hit, -jnp.inf, s)
        rank = jnp.where(hit, float(k), rank)
        top = jnp.where(slot == k, m, top)
    return top, rank


def _pair_rows(v0, v1, op, pad):
    t = v0.shape[1]
    b_idx = lax.broadcasted_iota(jnp.int32, (8, t), 0)
    groups = [op(v0[0:1], v1), op(v0[1:2], v1[0:8])]
    for a in range(2, 8):
        groups.append(jnp.where(b_idx < _PAIR_COUNTS[a], op(v0[a:a + 1], v1[0:8]), pad))
    groups.append(op(v0[8:16], v1[0:1]))
    return jnp.concatenate(groups, axis=0)


def _peer_topk_kernel(h2_ref, wq_ref, keys_ref, a_ref, nb_ref, bv_ref, r1_ref):
    q = _dot(h2_ref[...], wq_ref[...])
    s0 = _dot_nt(keys_ref[0, 0], q[:, :PK_HALF].astype(BF16))
    s1 = _dot_nt(keys_ref[0, 1], q[:, PK_HALF:].astype(BF16))
    top0, rank0 = _top16_rows(s0)
    top1, rank1 = _top16_rows(s1)
    t = s0.shape[1]

    cand0 = _pair_rows(top0, top1, jnp.add, -jnp.inf)
    n_c = cand0.shape[0]
    pos = lax.broadcasted_iota(jnp.int32, (n_c, t), 0).astype(F32)
    cand = cand0
    for _ in range(PEER_TOPK):
        m = jnp.max(cand, axis=0, keepdims=True)
        first = jnp.min(jnp.where(cand == m, pos, float(n_c)), axis=0, keepdims=True)
        cand = jnp.where(pos == first, -jnp.inf, cand)
    sel = cand != cand0

    e0 = jnp.exp(top0 - top0[0:1])
    e1 = jnp.exp(top1 - top1[0:1])
    pair_e = _pair_rows(e0, e1, jnp.multiply, 0.0)
    z = jnp.sum(jnp.where(sel, pair_e, 0.0), axis=0, keepdims=True)
    inv_z = 1.0 / z

    self_f = jnp.where(sel, 1.0, 0.0)
    row8 = lax.broadcasted_iota(jnp.int32, (8, t), 0)
    counts = jnp.zeros((8, t), F32)
    bounds = [(0, 16)] + [(16 + 8 * (a - 1), 24 + 8 * (a - 1)) for a in range(1, 8)]
    for a, (lo, hi) in enumerate(bounds):
        counts = jnp.where(row8 == a, jnp.sum(self_f[lo:hi], axis=0, keepdims=True), counts)
    nb16 = jnp.concatenate([counts, self_f[72:80]], axis=0)

    nb = jnp.zeros_like(rank0)
    for a in range(PEER_TOPK):
        nb = jnp.where(rank0 == float(a), nb16[a:a + 1], nb)
    a_ref[0] = jnp.exp(s0 - top0[0:1]) * inv_z
    nb_ref[0] = nb
    bv_ref[0] = jnp.exp(s1 - top1[0:1]).astype(BF16)
    r1_ref[0] = rank1.astype(BF16)


def _peer_topk(h2, wq_bf, keys_bf, tt=256):
    t = h2.shape[0]
    ospec = pl.BlockSpec((1, N_KEYS, tt), lambda i, h: (h, 0, i))
    return pl.pallas_call(
        _peer_topk_kernel,
        out_shape=[jax.ShapeDtypeStruct((PEER_HEADS, N_KEYS, t), dt) for dt in (F32, F32, BF16, BF16)],
        grid=(t // tt, PEER_HEADS),
        in_specs=[pl.BlockSpec((tt, D_MODEL), lambda i, h: (i, 0)),
                  pl.BlockSpec((D_MODEL, PK_DIM), lambda i, h: (0, h)),
                  pl.BlockSpec((1, 2, N_KEYS, PK_HALF), lambda i, h: (h, 0, 0, 0))],
        out_specs=[ospec, ospec, ospec, ospec],
        compiler_params=_params("parallel", "parallel"),
        name="peer_topk",
    )(h2, wq_bf, keys_bf)


def _peer_mix_kernel(h2_ref, u_ref, vt_ref, a_ref, nb_ref, bv_ref, r1_ref, x1_ref, mod_ref, g_ref,
                     y_ref, acc_ref, s0_ref, s1_ref, w0_ref, w1_ref, *, key_rows):
    e = pl.program_id(1)
    n_tiles = pl.num_programs(1) - 2
    tt = h2_ref.shape[0]
    tile_b = jnp.clip(e - 1, 0, n_tiles - 1)

    @pl.when(e == 0)
    def _():
        acc_ref[...] = jnp.zeros_like(acc_ref)
        s1_ref[...] = jnp.zeros_like(s1_ref)
        w0_ref[...] = jnp.zeros_like(w0_ref)

    zero = jnp.zeros((), BF16)
    sub = 16

    def stages(s_new, s_cur, w_new, w_old):
        def weights(r, cols):
            i = tile_b * key_rows + r
            rows = slice(r * N_KEYS, (r + 1) * N_KEYS)
            gate = jnp.zeros((N_KEYS, GATE_LANES), BF16)
            for h in range(PEER_HEADS):
                a_row = jnp.broadcast_to(a_ref[h, pl.ds(i, 1), cols], (sub, GATE_LANES)).astype(BF16)
                nb_row = jnp.broadcast_to(nb_ref[h, pl.ds(i, 1), cols], (sub, GATE_LANES)).astype(BF16)
                a_full = jnp.concatenate([a_row] * (N_KEYS // sub), axis=0)
                nb_full = jnp.concatenate([nb_row] * (N_KEYS // sub), axis=0)
                gate = gate + a_full * jnp.where(r1_ref[h, :, cols] < nb_full, bv_ref[h, :, cols], zero)
            act = jax.nn.gelu(s_cur[rows, cols]).astype(BF16)
            w_new[rows, cols] = gate * act

        def apply(k, cols):
            ks = slice(k * MXU_DIM, (k + 1) * MXU_DIM)
            acc_ref[:, cols] += _dot(vt_ref[:, ks], w_old[ks, cols])

        for c in range(tt // GATE_LANES):
            cols = slice(c * GATE_LANES, (c + 1) * GATE_LANES)
            s_new[:, cols] = _dot_nt(u_ref[...], h2_ref[cols, :])
            for k in range(key_rows * N_KEYS // MXU_DIM):
                apply(k, cols)
                for r in range(k * MXU_DIM // N_KEYS, (k + 1) * MXU_DIM // N_KEYS):
                    weights(r, cols)

    @pl.when(e % 2 == 0)
    def _():
        stages(s0_ref, s1_ref, w1_ref, w0_ref)

    @pl.when(e % 2 == 1)
    def _():
        stages(s1_ref, s0_ref, w0_ref, w1_ref)

    @pl.when(e == n_tiles + 1)
    def _():
        x2 = x1_ref[...] + mod_ref[0, 5:6, :] * acc_ref[...].T
        y_ref[...] = _rms(x2, g_ref[...])


def _peer_mix(h2, u_bf, vt_bf, a, nb, bv, r1, x1, mod, final_g, tiles_per_mod_row, tt=512, key_rows=4):
    t = h2.shape[0]
    et = key_rows * N_KEYS
    n_tiles = N_EXPERTS // et
    tok = pl.BlockSpec((tt, D_MODEL), lambda i, e: (i, 0))
    sel = pl.BlockSpec((PEER_HEADS, N_KEYS, tt), lambda i, e: (0, 0, i))
    return pl.pallas_call(
        functools.partial(_peer_mix_kernel, key_rows=key_rows),
        out_shape=jax.ShapeDtypeStruct((t, D_MODEL), F32),
        grid=(t // tt, n_tiles + 2),
        in_specs=[tok,
                  pl.BlockSpec((et, D_MODEL), lambda i, e: (jnp.minimum(e, n_tiles - 1), 0)),
                  pl.BlockSpec((D_MODEL, et), lambda i, e: (0, jnp.clip(e - 2, 0, n_tiles - 1))),
                  sel, sel, sel, sel,
                  tok,
                  pl.BlockSpec((1, 6, D_MODEL), lambda i, e: (i // tiles_per_mod_row, 0, 0)),
                  pl.BlockSpec((1, D_MODEL), lambda i, e: (0, 0))],
        out_specs=tok,
        scratch_shapes=[pltpu.VMEM((D_MODEL, tt), F32),
                        pltpu.VMEM((et, tt), F32), pltpu.VMEM((et, tt), F32),
                        pltpu.VMEM((et, tt), BF16), pltpu.VMEM((et, tt), BF16)],
        compiler_params=_params("parallel", "arbitrary"),
        name="peer_mix",
    )(h2, u_bf, vt_bf, a, nb, bv, r1, x1, mod, final_g)


def _tiles_per_mod_row(n_mod_rows, n_tokens, seq, tile):
    return seq // tile if n_mod_rows > 1 else n_tokens // tile


def _trunk(x, mod, w, batch, seq, cache):
    t = batch * seq
    n_mod = mod.shape[0]
    q, k, v, xr, xg, ga, gr = _inproj(x, mod, w["norm_attn_g"], w["w_in"],
                                      _tiles_per_mod_row(n_mod, t, seq, 256))
    if cache is None:
        attn = _attention(q, None, None, k, v, w["attn_sink"], batch, seq, seq, None)
        h0f = jnp.zeros((batch, LRU_WIDTH), F32)
        h0b = jnp.zeros((batch, LRU_WIDTH), F32)
    else:
        k_ctx, v_ctx, h0f, h0b = cache
        attn = _attention(q, k, v, k_ctx, v_ctx, w["attn_sink"], batch, seq,
                          k_ctx.shape[0] // batch, _rope_tables(seq))
    rnn, hf_last, hb_last = _rnn(xr, xg, w["conv_w"], w["conv_b"], w["lru_w4"], w["lru_b4"],
                                 w["lru_lambda"], h0f, h0b, batch, seq)
    x1, h2 = _outproj(x, attn, rnn, ga, gr, mod, w["norm_ffn_g"], w["w_attn_o"], w["w_rnn_o"],
                      w["w_out"], _tiles_per_mod_row(n_mod, t, seq, 256))
    a, nb, bv, r1 = _peer_topk(h2, w["peer_w_q"], w["peer_keys"])
    y = _peer_mix(h2, w["peer_u"], w["peer_vt"], a, nb, bv, r1, x1, mod, w["final_g"],
                  _tiles_per_mod_row(n_mod, t, seq, 512))
    return y, k, v, hf_last, hb_last


def kernel(x_prompt, x_sample, c, cache_k, cache_v, state_h_fwd, state_h_bwd, c_ctx, norm_attn_g, norm_ffn_g, final_g, w_ada, b_ada, w_in, attn_sink, conv_w, conv_b, lru_w_a, lru_b_a, lru_w_i, lru_b_i, lru_lambda, w_attn_o, w_rnn_o, w_out, peer_w_q, peer_keys, peer_u, peer_v):
    depth = w_in.shape[0]
    assert depth == 1, "single trunk layer"
    batch, seq, _ = x_prompt.shape
    dec_batch, dec_seq, _ = x_sample.shape
    past_len = cache_k.shape[2]
    l = 0

    cond = jnp.concatenate([jnp.broadcast_to(c_ctx[None, :], (8, D_MODEL)), c], axis=0)
    mod = _ada(cond, w_ada[l], b_ada[l][None, :]).reshape(-1, 6, D_MODEL)
    mod_ctx, mod_lat = mod[0:1], mod[8:]

    lru_w4 = jnp.concatenate([lru_w_a[l, 0], lru_w_i[l, 0], lru_w_a[l, 1], lru_w_i[l, 1]], axis=-1).astype(BF16)
    lru_b4 = jnp.stack([lru_b_a[l, 0], lru_b_i[l, 0], lru_b_a[l, 1], lru_b_i[l, 1]], axis=0)
    lru_b4 = lru_b4.reshape(4, LRU_BLOCKS, LRU_BLOCK_W).transpose(1, 0, 2).reshape(1, 4 * LRU_WIDTH)

    w = {
        "norm_attn_g": norm_attn_g[l][None, :], "norm_ffn_g": norm_ffn_g[l][None, :], "final_g": final_g[None, :],
        "w_in": w_in[l].astype(BF16), "attn_sink": attn_sink[l],
        "conv_w": conv_w[l], "conv_b": conv_b[l][None, :],
        "lru_w4": lru_w4, "lru_b4": lru_b4, "lru_lambda": lru_lambda[l],
        "w_attn_o": w_attn_o[l].astype(BF16), "w_rnn_o": w_rnn_o[l].astype(BF16), "w_out": w_out[l].astype(BF16),
        "peer_w_q": peer_w_q[l].astype(BF16), "peer_keys": peer_keys[l].astype(BF16),
        "peer_u": peer_u[l].astype(BF16), "peer_vt": peer_v[l].T.astype(BF16),
    }

    yp, k_l, v_l, hf_l, hb_l = _trunk(x_prompt.reshape(batch * seq, D_MODEL), mod_ctx, w, batch, seq, None)
    cache = (cache_k[:, l].reshape(dec_batch * past_len, KV_WIDTH),
             cache_v[:, l].reshape(dec_batch * past_len, KV_WIDTH),
             state_h_fwd[:, l], state_h_bwd[:, l])
    ys, _, _, _, _ = _trunk(x_sample.reshape(dec_batch * dec_seq, D_MODEL), mod_lat, w, dec_batch, dec_seq, cache)

    return (yp.reshape(batch, seq, D_MODEL),
            ys.reshape(dec_batch, dec_seq, D_MODEL),
            k_l.reshape(batch, 1, seq, KV_HEADS, HEAD_DIM),
            v_l.reshape(batch, 1, seq, KV_HEADS, HEAD_DIM),
            hf_l[:, None, :],
            hb_l[:, None, :])
```
